```python
import jax, jax.numpy as jnp
from jax import lax
import numpy as np

D_MODEL = 1024
BATCH = 2
SEQ = 16384
DEPTH = 4

D_FF = 2816
LRU_WIDTH = 1024
LRU_HEADS = 4
LRU_HEAD_DIM = LRU_WIDTH // LRU_HEADS
LRU_CONV = 4
LRU_PAD = (2, 1)
LRU_C = 8.0
SC_WIDTH = 512
SC_CONV = 3
SC_PAD = (1, 1)
SGU_WIDTH = 512
SGU_HEADS = 4
SGU_HEAD_DIM = SGU_WIDTH // SGU_HEADS
CHUNK = 128
N_BRANCH = 3
EPS = 1e-6

_PART = (LRU_WIDTH, LRU_WIDTH, SC_WIDTH, SC_WIDTH, SC_WIDTH,
         SGU_WIDTH, SGU_WIDTH, N_BRANCH * D_MODEL)
D_IN = sum(_PART)
SPLIT_POINTS = tuple(int(p) for p in np.cumsum(_PART)[:-1])

kernel_name = "hybrid_rglru_shortconv_sgu_encoder"


def rmsnorm(x, g):
    xf = x.astype(jnp.float32)
    y = xf * lax.rsqrt(jnp.mean(xf * xf, axis=-1, keepdims=True) + EPS)
    return y.astype(x.dtype) * g


def layernorm(x, g, b):
    xf = x.astype(jnp.float32)
    mu = jnp.mean(xf, axis=-1, keepdims=True)
    var = jnp.mean(jnp.square(xf - mu), axis=-1, keepdims=True)
    return ((xf - mu) * lax.rsqrt(var + EPS)).astype(x.dtype) * g + b


def swiglu(h, w_gate, w_up, w_down):
    return (jax.nn.silu(h @ w_gate) * (h @ w_up)) @ w_down


def depthwise_conv(x, w, pad):
    c = x.shape[-1]
    return lax.conv_general_dilated(
        x, w[:, None, :], window_strides=(1,), padding=[pad],
        dimension_numbers=("NWC", "WIO", "NWC"), feature_group_count=c)


def _lin_combine(left, right):
    a_l, b_l = left
    a_r, b_r = right
    return a_l * a_r, a_r * b_l + b_r


def rg_lru(x, w_a, b_a, w_x, b_x, lam, reverse):
    bsz, s, wdt = x.shape
    xh = x.reshape(bsz, s, LRU_HEADS, LRU_HEAD_DIM)
    r = jax.nn.sigmoid(jnp.einsum("bshd,hde->bshe", xh, w_a).reshape(bsz, s, wdt) + b_a)
    i = jax.nn.sigmoid(jnp.einsum("bshd,hde->bshe", xh, w_x).reshape(bsz, s, wdt) + b_x)
    log_a = (-LRU_C * jax.nn.softplus(-lam.astype(jnp.float32))) * r.astype(jnp.float32)
    a = jnp.exp(log_a)
    u = (i * x).astype(jnp.float32) * jnp.sqrt(-jnp.expm1(2.0 * log_a))
    _, h = lax.associative_scan(_lin_combine, (a, u), reverse=reverse, axis=1)
    return h.astype(x.dtype)


def spatial_gating(u, v, ln_g, ln_b, w_s, b_s):
    u = jax.nn.gelu(u)
    v = layernorm(jax.nn.gelu(v), ln_g, ln_b)
    bsz, s, _ = v.shape
    vc = v.reshape(bsz, s // CHUNK, CHUNK, SGU_HEADS, SGU_HEAD_DIM)
    mixed = jnp.einsum("gpq,bnqgc->bnpgc", w_s, vc) + b_s.T[:, :, None]
    return u * mixed.reshape(bsz, s, SGU_WIDTH)


def mixer_block(h, w_in, lru_conv_w, lru_conv_b, lru_wa, lru_ba, lru_wx, lru_bx,
                lru_lambda, lru_w_out, sc_conv_w, sc_w_out, sgu_ln_g, sgu_ln_b,
                sgu_w_s, sgu_b, sgu_w_out, w_o):
    z = h @ w_in
    lru_gate, lru_x, sc_b, sc_c, sc_x, sgu_u, sgu_v, merge = jnp.split(z, SPLIT_POINTS, axis=-1)
    xc = depthwise_conv(lru_x, lru_conv_w, LRU_PAD) + lru_conv_b
    h_fwd = rg_lru(xc, lru_wa[0], lru_ba[0], lru_wx[0], lru_bx[0], lru_lambda[0], reverse=False)
    h_bwd = rg_lru(xc, lru_wa[1], lru_ba[1], lru_wx[1], lru_bx[1], lru_lambda[1], reverse=True)
    y_a = ((h_fwd + h_bwd) * jax.nn.gelu(lru_gate)) @ lru_w_out
    y_b = (sc_b * depthwise_conv(sc_c * sc_x, sc_conv_w, SC_PAD)) @ sc_w_out
    y_c = spatial_gating(sgu_u, sgu_v, sgu_ln_g, sgu_ln_b, sgu_w_s, sgu_b) @ sgu_w_out
    g = jax.nn.sigmoid(merge).reshape(*merge.shape[:-1], N_BRANCH, D_MODEL)
    m = g[..., 0, :] * y_a + g[..., 1, :] * y_b + g[..., 2, :] * y_c
    return m @ w_o


def setup_inputs(seed: int = 0) -> dict:
    key = jax.random.key(seed)
    ks = iter(jax.random.split(key, 32))
    L, D, F = DEPTH, D_MODEL, D_FF

    def w(shape, fan_in):
        return jax.random.normal(next(ks), shape, jnp.float32) * (fan_in ** -0.5)

    def gain(shape):
        return 1.0 + 0.02 * jax.random.normal(next(ks), shape, jnp.float32)

    def bias(shape):
        return 0.02 * jax.random.normal(next(ks), shape, jnp.float32)

    a0 = jax.random.uniform(next(ks), (L, 2, LRU_WIDTH), jnp.float32, 0.9, 0.999)
    s0 = a0 ** (1.0 / LRU_C)
    lru_lambda = jnp.log(s0) - jnp.log1p(-s0)
    return {
        "x": jax.random.normal(next(ks), (BATCH, SEQ, D), jnp.float32),
        "ffn1_pre_g": gain((L, D)),
        "ffn1_w_gate": w((L, D, F), D),
        "ffn1_w_up": w((L, D, F), D),
        "ffn1_w_down": w((L, F, D), F),
        "ffn1_post_g": gain((L, D)),
        "mix_pre_g": gain((L, D)),
        "w_in": w((L, D, D_IN), D),
        "lru_conv_w": w((L, LRU_CONV, LRU_WIDTH), LRU_CONV),
        "lru_conv_b": bias((L, LRU_WIDTH)),
        "lru_wa": w((L, 2, LRU_HEADS, LRU_HEAD_DIM, LRU_HEAD_DIM), LRU_HEAD_DIM),
        "lru_ba": bias((L, 2, LRU_WIDTH)),
        "lru_wx": w((L, 2, LRU_HEADS, LRU_HEAD_DIM, LRU_HEAD_DIM), LRU_HEAD_DIM),
        "lru_bx": bias((L, 2, LRU_WIDTH)),
        "lru_lambda": lru_lambda,
        "lru_w_out": w((L, LRU_WIDTH, D), LRU_WIDTH),
        "sc_conv_w": w((L, SC_CONV, SC_WIDTH), SC_CONV),
        "sc_w_out": w((L, SC_WIDTH, D), SC_WIDTH),
        "sgu_ln_g": gain((L, SGU_WIDTH)),
        "sgu_ln_b": bias((L, SGU_WIDTH)),
        "sgu_w_s": w((L, SGU_HEADS, CHUNK, CHUNK), CHUNK),
        "sgu_b": bias((L, SGU_HEADS, CHUNK)),
        "sgu_w_out": w((L, SGU_WIDTH, D), SGU_WIDTH),
        "w_o": w((L, D, D), D),
        "mix_post_g": gain((L, D)),
        "ffn2_pre_g": gain((L, D)),
        "ffn2_w_gate": w((L, D, F), D),
        "ffn2_w_up": w((L, D, F), D),
        "ffn2_w_down": w((L, F, D), F),
        "ffn2_post_g": gain((L, D)),
    }


def reference(x, ffn1_pre_g, ffn1_w_gate, ffn1_w_up, ffn1_w_down, ffn1_post_g,
              mix_pre_g, w_in, lru_conv_w, lru_conv_b, lru_wa, lru_ba, lru_wx, lru_bx,
              lru_lambda, lru_w_out, sc_conv_w, sc_w_out, sgu_ln_g, sgu_ln_b,
              sgu_w_s, sgu_b, sgu_w_out, w_o, mix_post_g,
              ffn2_pre_g, ffn2_w_gate, ffn2_w_up, ffn2_w_down, ffn2_post_g):
    for l in range(DEPTH):
        f1 = swiglu(rmsnorm(x, ffn1_pre_g[l]), ffn1_w_gate[l], ffn1_w_up[l], ffn1_w_down[l])
        x = x + 0.5 * rmsnorm(f1, ffn1_post_g[l])
        mx = mixer_block(rmsnorm(x, mix_pre_g[l]), w_in[l], lru_conv_w[l], lru_conv_b[l],
                         lru_wa[l], lru_ba[l], lru_wx[l], lru_bx[l], lru_lambda[l],
                         lru_w_out[l], sc_conv_w[l], sc_w_out[l], sgu_ln_g[l], sgu_ln_b[l],
                         sgu_w_s[l], sgu_b[l], sgu_w_out[l], w_o[l])
        x = x + rmsnorm(mx, mix_post_g[l])
        f2 = swiglu(rmsnorm(x, ffn2_pre_g[l]), ffn2_w_gate[l], ffn2_w_up[l], ffn2_w_down[l])
        x = x + 0.5 * rmsnorm(f2, ffn2_post_g[l])
    return x
```

```python
import functools

import jax
import jax.numpy as jnp
from jax import lax
from jax.experimental import pallas as pl
from jax.experimental.pallas import tpu as pltpu

F32 = jnp.float32
BF16 = jnp.bfloat16

EPS = 1e-6
LRU_C = 8.0
LRU_HEADS = 4
SGU_HEADS = 4
CHUNK = 128

LANES = 128
SUBLANES = 8
HALO = SUBLANES
VMEM_LIMIT = 56 * 1024 * 1024

TM_FFN = 512
TM_MIX = 512
TS_SCAN = 512
SEG_PITCH_PAD = 8


def _rms_unit(x):
    return x * lax.rsqrt(jnp.mean(x * x, axis=-1, keepdims=True) + EPS)


def _sigmoid(x):
    return 1.0 / (1.0 + jnp.exp(-x))


def _gelu(x):
    return 0.5 * x * (1.0 + jnp.tanh(0.7978845608028654 * (x + 0.044715 * (x * x * x))))


def _dot(a, b):
    return jnp.dot(a, b, preferred_element_type=F32)


def _const_spec(shape):
    nd = len(shape)
    return pl.BlockSpec(shape, lambda *_: (0,) * nd, pipeline_mode=pl.Buffered(1))


def _params(*sem):
    return pltpu.CompilerParams(dimension_semantics=sem, vmem_limit_bytes=VMEM_LIMIT)


def _ffn_kernel(x_ref, pre_g_ref, wg_ref, wu_ref, wd_ref, post_g_ref, o_ref):
    x = x_ref[...]
    hn = (_rms_unit(x) * pre_g_ref[...]).astype(BF16)
    g = _dot(hn, wg_ref[...])
    u = _dot(hn, wu_ref[...])
    act = (g * _sigmoid(g) * u).astype(BF16)
    f = _dot(act, wd_ref[...])
    o_ref[...] = x + 0.5 * (_rms_unit(f) * post_g_ref[...])


def _ffn(x, pre_g, wg, wu, wd, post_g):
    n, d = x.shape
    f = wg.shape[1]
    tm = min(TM_FFN, n)
    row = pl.BlockSpec((tm, d), lambda i: (i, 0))
    return pl.pallas_call(
        _ffn_kernel,
        grid=(n // tm,),
        in_specs=[row, _const_spec((1, d)), _const_spec((d, f)), _const_spec((d, f)),
                  _const_spec((f, d)), _const_spec((1, d))],
        out_specs=row,
        out_shape=jax.ShapeDtypeStruct((n, d), F32),
        compiler_params=_params("parallel"),
        name="ffn",
    )(x, pre_g, wg, wu, wd, post_g)


def _mix_in_kernel(x_ref, g_ref, w_in_ref, ln_g_ref, ln_b_ref, ws_ref, bs_ref,
                   gate_ref, lrux_ref, scb_ref, sccx_ref, yc_ref, mg_ref, *, lw, sw, gw):
    tm = x_ref.shape[0]
    hn = (_rms_unit(x_ref[...]) * g_ref[...]).astype(BF16)
    o = 0
    gate_ref[...] = _gelu(_dot(hn, w_in_ref[:, o:o + lw])).astype(BF16)
    o += lw
    lrux_ref[...] = _dot(hn, w_in_ref[:, o:o + lw])
    o += lw
    zsc = _dot(hn, w_in_ref[:, o:o + 3 * sw])
    o += 3 * sw
    scb_ref[...] = zsc[:, :sw].astype(BF16)
    sccx_ref[...] = zsc[:, sw:2 * sw] * zsc[:, 2 * sw:]
    zs = _dot(hn, w_in_ref[:, o:o + 2 * gw])
    o += 2 * gw
    u = _gelu(zs[:, :gw])
    v = _gelu(zs[:, gw:])
    mu = jnp.mean(v, axis=-1, keepdims=True)
    vc = v - mu
    var = jnp.mean(vc * vc, axis=-1, keepdims=True)
    vn = ((vc * lax.rsqrt(var + EPS)) * ln_g_ref[...] + ln_b_ref[...]).astype(BF16)
    hd = gw // SGU_HEADS
    for n in range(tm // CHUNK):
        rows = slice(n * CHUNK, (n + 1) * CHUNK)
        mixed = jnp.concatenate(
            [_dot(ws_ref[h], vn[rows, h * hd:(h + 1) * hd]) for h in range(SGU_HEADS)], axis=-1)
        yc_ref[rows, :] = (u[rows, :] * (mixed + bs_ref[...])).astype(BF16)
    mg_ref[...] = _sigmoid(_dot(hn, w_in_ref[:, o:])).astype(BF16)


def _mix_in(x, g, w_in, ln_g, ln_b, ws, bs, *, lw, sw, gw):
    n, d = x.shape
    d_in = w_in.shape[1]
    tm = min(TM_MIX, n)

    def row(w):
        return pl.BlockSpec((tm, w), lambda i: (i, 0))

    return pl.pallas_call(
        functools.partial(_mix_in_kernel, lw=lw, sw=sw, gw=gw),
        grid=(n // tm,),
        in_specs=[row(d), _const_spec((1, d)), _const_spec((d, d_in)), _const_spec((1, gw)),
                  _const_spec((1, gw)), _const_spec(ws.shape), _const_spec(bs.shape)],
        out_specs=[row(lw), row(lw), row(sw), row(sw), row(gw), row(3 * d)],
        out_shape=[jax.ShapeDtypeStruct((n, lw), BF16), jax.ShapeDtypeStruct((n, lw), F32),
                   jax.ShapeDtypeStruct((n, sw), BF16), jax.ShapeDtypeStruct((n, sw), F32),
                   jax.ShapeDtypeStruct((n, gw), BF16), jax.ShapeDtypeStruct((n, 3 * d), BF16)],
        compiler_params=_params("parallel"),
        name="mix_in",
    )(x, g, w_in, ln_g, ln_b, ws, bs)


def _fill_ext(ext_ref, main, prev, nxt, first, last):
    rows = main.shape[0]
    for c in range(ext_ref.shape[0]):
        lanes = slice(c * LANES, (c + 1) * LANES)
        ext_ref[c, 0:HALO, :] = jnp.where(first, 0.0, prev[:, lanes])
        ext_ref[c, HALO:HALO + rows, :] = main[:, lanes]
        ext_ref[c, HALO + rows:, :] = jnp.where(last, 0.0, nxt[:, lanes])


def _conv_chunk(ext_ref, c, rows, w, left):
    acc = None
    for k in range(w.shape[0]):
        term = ext_ref[c, pl.ds(HALO - left + k, rows), :] * w[k:k + 1, :]
        acc = term if acc is None else acc + term
    return acc


def _halo_specs(tm, width, n_tiles, tile_of):
    per = tm // HALO
    last_blk = n_tiles * per - 1
    main = pl.BlockSpec((tm, width), lambda *g: (tile_of(*g), 0))
    prev = pl.BlockSpec((HALO, width), lambda *g: (jnp.maximum(tile_of(*g) * per - 1, 0), 0))
    nxt = pl.BlockSpec((HALO, width), lambda *g: (jnp.minimum((tile_of(*g) + 1) * per, last_blk), 0))
    return main, prev, nxt


def _neg_expm1(y):
    series = -y * (1.0 + y * (0.5 + y * (1.0 / 6.0 + y * (1.0 / 24.0 + y * (1.0 / 120.0 + y * (1.0 / 720.0))))))
    return jnp.where(y > -0.0625, series, 1.0 - jnp.exp(y))


def _scan_kernel(x_ref, xp_ref, xn_ref, cw_ref, cb_ref, wa_ref, ba_ref, wx_ref, bx_ref, lam_ref,
                 h_ref, ext_ref, a_ref, u_ref, hl_ref, pc_ref, carry_ref, *, reverse, n_tiles):
    j = pl.program_id(1)
    jj = n_tiles - 1 - j if reverse else j
    ts = x_ref.shape[0]
    nch = ext_ref.shape[0]
    seg = ts // SUBLANES
    pitch = seg + SEG_PITCH_PAD
    hd_ch = nch // LRU_HEADS

    @pl.when(j == 0)
    def _():
        carry_ref[...] = jnp.zeros_like(carry_ref)

    _fill_ext(ext_ref, x_ref[...], xp_ref[...], xn_ref[...], jj == 0, jj == n_tiles - 1)

    lam = lam_ref[...]
    neg_c_sp = -LRU_C * (jnp.maximum(-lam, 0.0) + jnp.log1p(jnp.exp(-jnp.abs(lam))))
    cw = cw_ref[...]
    for hd in range(LRU_HEADS):
        lanes = slice(hd * hd_ch * LANES, (hd + 1) * hd_ch * LANES)
        xc = jnp.concatenate(
            [_conv_chunk(ext_ref, c, ts, cw[:, c * LANES:(c + 1) * LANES], 2)
             for c in range(hd * hd_ch, (hd + 1) * hd_ch)], axis=-1) + cb_ref[:, lanes]
        xb = xc.astype(BF16)
        r = _sigmoid(_dot(xb, wa_ref[hd]) + ba_ref[:, lanes])
        i = _sigmoid(_dot(xb, wx_ref[hd]) + bx_ref[:, lanes])
        log_a = neg_c_sp[:, lanes] * r
        a = jnp.exp(log_a)
        u = (i * xc) * jnp.sqrt(_neg_expm1(2.0 * log_a))
        for cc in range(hd_ch):
            c = hd * hd_ch + cc
            for s in range(SUBLANES):
                a_ref[c, s * pitch:s * pitch + seg, :] = a[s * seg:(s + 1) * seg, cc * LANES:(cc + 1) * LANES]
                u_ref[c, s * pitch:s * pitch + seg, :] = u[s * seg:(s + 1) * seg, cc * LANES:(cc + 1) * LANES]

    def step(it, hp):
        k = seg - 1 - it if reverse else it
        out = []
        for c in range(nch):
            av = a_ref[c, pl.ds(k, SUBLANES, stride=pitch), :]
            uv = u_ref[c, pl.ds(k, SUBLANES, stride=pitch), :]
            h = av * hp[c] + uv
            p = av * hp[nch + c]
            row = pl.multiple_of(k * SUBLANES, SUBLANES)
            hl_ref[c, pl.ds(row, SUBLANES), :] = h
            pc_ref[c, pl.ds(row, SUBLANES), :] = p
            out.append((h, p))
        return tuple(o[0] for o in out) + tuple(o[1] for o in out)

    zeros = jnp.zeros((SUBLANES, LANES), F32)
    ones = jnp.ones((SUBLANES, LANES), F32)
    fin = lax.fori_loop(0, seg, step, (zeros,) * nch + (ones,) * nch)

    sub = lax.broadcasted_iota(jnp.int32, (SUBLANES, LANES), 0)
    order = list(range(SUBLANES - 1, -1, -1) if reverse else range(SUBLANES))
    for c in range(nch):
        h_end, p_end = fin[c], fin[nch + c]
        cin = carry_ref[c]
        for s, nxt_s in zip(order[:-1], order[1:]):
            cand = p_end * cin + h_end
            cin = jnp.where(sub == nxt_s, pltpu.roll(cand, (nxt_s - s) % SUBLANES, 0), cin)
        last_s = order[-1]
        tail = p_end * cin + h_end
        carry_ref[c] = jnp.broadcast_to(tail[last_s:last_s + 1, :], (SUBLANES, LANES))
        for s in range(SUBLANES):
            cs = jnp.broadcast_to(cin[s:s + 1, :], (SUBLANES, LANES))
            for k0 in range(seg // SUBLANES):
                src = pl.ds(k0 * SUBLANES * SUBLANES + s, SUBLANES, stride=SUBLANES)
                val = hl_ref[c, src, :] + pc_ref[c, src, :] * cs
                r0 = s * seg + k0 * SUBLANES
                h_ref[r0:r0 + SUBLANES, c * LANES:(c + 1) * LANES] = val


def _scan(x, bsz, cw, cb, wa, ba, wx, bx, lam, *, reverse):
    n, w = x.shape
    s_len = n // bsz
    ts = min(TS_SCAN, s_len)
    nt = s_len // ts
    nch = w // LANES
    seg = ts // SUBLANES
    pitch = seg + SEG_PITCH_PAD

    def tile_of(b, j):
        return b * nt + (nt - 1 - j if reverse else j)

    per = ts // HALO
    main = pl.BlockSpec((ts, w), lambda b, j: (tile_of(b, j), 0))
    prev = pl.BlockSpec((HALO, w), lambda b, j: (jnp.maximum(tile_of(b, j) * per - 1, 0), 0))
    nxt = pl.BlockSpec((HALO, w), lambda b, j: (jnp.minimum((tile_of(b, j) + 1) * per, n // HALO - 1), 0))
    return pl.pallas_call(
        functools.partial(_scan_kernel, reverse=reverse, n_tiles=nt),
        grid=(bsz, nt),
        in_specs=[main, prev, nxt, _const_spec(cw.shape), _const_spec((1, w)), _const_spec(wa.shape),
                  _const_spec((1, w)), _const_spec(wx.shape), _const_spec((1, w)), _const_spec((1, w))],
        out_specs=main,
        out_shape=jax.ShapeDtypeStruct((n, w), F32),
        scratch_shapes=[pltpu.VMEM((nch, ts + 2 * HALO, LANES), F32),
                        pltpu.VMEM((nch, SUBLANES * pitch, LANES), F32),
                        pltpu.VMEM((nch, SUBLANES * pitch, LANES), F32),
                        pltpu.VMEM((nch, ts, LANES), F32),
                        pltpu.VMEM((nch, ts, LANES), F32),
                        pltpu.VMEM((nch, SUBLANES, LANES), F32)],
        compiler_params=_params("arbitrary", "arbitrary"),
        name="scan_bwd" if reverse else "scan_fwd",
    )(x, x, x, cw, cb, wa, ba, wx, bx, lam)


def _mix_out_kernel(x_ref, gate_ref, hf_ref, hb_ref, scb_ref, cx_ref, cxp_ref, cxn_ref, yc_ref, mg_ref,
                    scw_ref, wlo_ref, wso_ref, wgo_ref, wo_ref, post_g_ref, o_ref, ext_ref, *, n_tiles):
    i = pl.program_id(0)
    jj = i % n_tiles
    tm, d = x_ref.shape
    ya = _dot(((hf_ref[...] + hb_ref[...]) * gate_ref[...].astype(F32)).astype(BF16), wlo_ref[...])
    _fill_ext(ext_ref, cx_ref[...], cxp_ref[...], cxn_ref[...], jj == 0, jj == n_tiles - 1)
    scw = scw_ref[...]
    conv = jnp.concatenate(
        [_conv_chunk(ext_ref, c, tm, scw[:, c * LANES:(c + 1) * LANES], 1) for c in range(ext_ref.shape[0])],
        axis=-1)
    yb = _dot((scb_ref[...].astype(F32) * conv).astype(BF16), wso_ref[...])
    yc = _dot(yc_ref[...], wgo_ref[...])
    m = (mg_ref[:, 0:d].astype(F32) * ya + mg_ref[:, d:2 * d].astype(F32) * yb
         + mg_ref[:, 2 * d:].astype(F32) * yc)
    out = _dot(m.astype(BF16), wo_ref[...])
    o_ref[...] = x_ref[...] + _rms_unit(out) * post_g_ref[...]


def _mix_out(x, bsz, gate, hf, hb, scb, sccx, yc, mg, scw, wlo, wso, wgo, wo, post_g):
    n, d = x.shape
    lw, sw, gw = gate.shape[1], scb.shape[1], yc.shape[1]
    s_len = n // bsz
    tm = min(TM_MIX, s_len)
    nt = s_len // tm

    def row(w):
        return pl.BlockSpec((tm, w), lambda i: (i, 0))

    per = tm // HALO
    prev = pl.BlockSpec((HALO, sw), lambda i: (jnp.maximum(i * per - 1, 0), 0))
    nxt = pl.BlockSpec((HALO, sw), lambda i: (jnp.minimum((i + 1) * per, n // HALO - 1), 0))
    return pl.pallas_call(
        functools.partial(_mix_out_kernel, n_tiles=nt),
        grid=(n // tm,),
        in_specs=[row(d), row(lw), row(lw), row(lw), row(sw), row(sw), prev, nxt, row(gw), row(3 * d),
                  _const_spec(scw.shape), _const_spec(wlo.shape), _const_spec(wso.shape),
                  _const_spec(wgo.shape), _const_spec(wo.shape), _const_spec((1, d))],
        out_specs=row(d),
        out_shape=jax.ShapeDtypeStruct((n, d), F32),
        scratch_shapes=[pltpu.VMEM((sw // LANES, tm + 2 * HALO, LANES), F32)],
        compiler_params=_params("parallel"),
        name="mix_out",
    )(x, gate, hf, hb, scb, sccx, sccx, sccx, yc, mg, scw, wlo, wso, wgo, wo, post_g)


def kernel(x, ffn1_pre_g, ffn1_w_gate, ffn1_w_up, ffn1_w_down, ffn1_post_g, mix_pre_g, w_in, lru_conv_w, lru_conv_b, lru_wa, lru_ba, lru_wx, lru_bx, lru_lambda, lru_w_out, sc_conv_w, sc_w_out, sgu_ln_g, sgu_ln_b, sgu_w_s, sgu_b, sgu_w_out, w_o, mix_post_g, ffn2_pre_g, ffn2_w_gate, ffn2_w_up, ffn2_w_down, ffn2_post_g):
    bsz, s_len, d = x.shape
    depth = w_in.shape[0]
    lw = lru_w_out.shape[1]
    sw = sc_w_out.shape[1]
    gw = sgu_w_out.shape[1]
    assert lw % (LRU_HEADS * LANES) == 0 and sw % LANES == 0 and gw == SGU_HEADS * LANES
    assert all(s_len % min(t, s_len) == 0 for t in (TM_FFN, TM_MIX, TS_SCAN)) and s_len % CHUNK == 0

    def vec(p, l):
        return p[l].reshape(1, -1)

    bf = lambda p: p.astype(BF16)
    xf = x.reshape(bsz * s_len, d)
    for l in range(depth):
        xf = _ffn(xf, vec(ffn1_pre_g, l), bf(ffn1_w_gate[l]), bf(ffn1_w_up[l]), bf(ffn1_w_down[l]),
                  vec(ffn1_post_g, l))
        bs = jnp.repeat(sgu_b[l].T, gw // SGU_HEADS, axis=1)
        gate, lrux, scb, sccx, yc, mg = _mix_in(
            xf, vec(mix_pre_g, l), bf(w_in[l]), vec(sgu_ln_g, l), vec(sgu_ln_b, l), bf(sgu_w_s[l]), bs,
            lw=lw, sw=sw, gw=gw)
        hs = [
            _scan(lrux, bsz, lru_conv_w[l], vec(lru_conv_b, l), bf(lru_wa[l, r]), lru_ba[l, r].reshape(1, -1),
                  bf(lru_wx[l, r]), lru_bx[l, r].reshape(1, -1), lru_lambda[l, r].reshape(1, -1),
                  reverse=bool(r))
            for r in range(2)]
        xf = _mix_out(xf, bsz, gate, hs[0], hs[1], scb, sccx, yc, mg, sc_conv_w[l], bf(lru_w_out[l]),
                      bf(sc_w_out[l]), bf(sgu_w_out[l]), bf(w_o[l]), vec(mix_post_g, l))
        xf = _ffn(xf, vec(ffn2_pre_g, l), bf(ffn2_w_gate[l]), bf(ffn2_w_up[l]), bf(ffn2_w_down[l]),
                  vec(ffn2_post_g, l))
    return xf.reshape(bsz, s_len, d)
```

```python
import functools

import jax
import jax.numpy as jnp
from jax import lax
from jax.experimental import pallas as pl
from jax.experimental.pallas import tpu as pltpu

F32 = jnp.float32
BF16 = jnp.bfloat16

EPS = 1e-6
LRU_C = 8.0
LRU_HEADS = 4
SGU_HEADS = 4
CHUNK = 128

LANES = 128
SUBLANES = 8
HALO = SUBLANES
VMEM_LIMIT = 56 * 1024 * 1024

TM_FFN = 512
TM_MIX = 256
SEG_PITCH_PAD = 8


def _rms_unit(x):
    return x * lax.rsqrt(jnp.mean(x * x, axis=-1, keepdims=True) + EPS)


def _sigmoid(x):
    return 0.5 * jnp.tanh(0.5 * x) + 0.5


def _silu(x):
    h = 0.5 * x
    return h + h * jnp.tanh(h)


def _gelu(x):
    h = 0.5 * x
    return h + h * jnp.tanh(x * (0.7978845608028654 + 0.035677408136300125 * (x * x)))


def _dot(a, b):
    return jnp.dot(a, b, preferred_element_type=F32)


def _const_spec(shape):
    nd = len(shape)
    return pl.BlockSpec(shape, lambda *_: (0,) * nd, pipeline_mode=pl.Buffered(1))


def _params(*sem):
    return pltpu.CompilerParams(dimension_semantics=sem, vmem_limit_bytes=VMEM_LIMIT)


def _ffn_tile(x, pre_g_ref, wg_ref, wu_ref, wd_ref, post_g_ref):
    hn = (_rms_unit(x) * pre_g_ref[...]).astype(BF16)
    act = (_silu(_dot(hn, wg_ref[...])) * _dot(hn, wu_ref[...])).astype(BF16)
    f = _dot(act, wd_ref[...])
    return x + 0.5 * (_rms_unit(f) * post_g_ref[...])


def _ffn_kernel(x_ref, pre_g_ref, wg_ref, wu_ref, wd_ref, post_g_ref, o_ref):
    o_ref[...] = _ffn_tile(x_ref[...], pre_g_ref, wg_ref, wu_ref, wd_ref, post_g_ref)


def _ffn(x, pre_g, wg, wu, wd, post_g):
    n, d = x.shape
    f = wg.shape[1]
    tm = min(TM_FFN, n)
    row = pl.BlockSpec((tm, d), lambda i: (i, 0))
    return pl.pallas_call(
        _ffn_kernel,
        grid=(n // tm,),
        in_specs=[row, _const_spec((1, d)), _const_spec((d, f)), _const_spec((d, f)),
                  _const_spec((f, d)), _const_spec((1, d))],
        out_specs=row,
        out_shape=jax.ShapeDtypeStruct((n, d), F32),
        compiler_params=_params("parallel"),
        name="ffn",
    )(x, pre_g, wg, wu, wd, post_g)


def _fill_ext(ext_ref, main, prev, nxt, first, last):
    rows = main.shape[0]
    for c in range(ext_ref.shape[0]):
        lanes = slice(c * LANES, (c + 1) * LANES)
        ext_ref[c, 0:HALO, :] = jnp.where(first, 0.0, prev[:, lanes])
        ext_ref[c, HALO:HALO + rows, :] = main[:, lanes]
        ext_ref[c, HALO + rows:, :] = jnp.where(last, 0.0, nxt[:, lanes])


def _conv(ext_ref, rows, w, left):
    outs = []
    for c in range(ext_ref.shape[0]):
        acc = None
        for k in range(w.shape[0]):
            term = ext_ref[c, pl.ds(HALO - left + k, rows), :] * w[k:k + 1, c * LANES:(c + 1) * LANES]
            acc = term if acc is None else acc + term
        outs.append(acc)
    return jnp.concatenate(outs, axis=-1)


def _rglru_head(hd, xc, wa_ref, ba_ref, wx_ref, bx_ref, lam_ref, h_ref, a_ref, u_ref, hl_ref, pc_ref,
                carry_ref, *, reverse):
    ts, width = xc.shape
    seg = ts // SUBLANES
    pitch = seg + SEG_PITCH_PAD
    hd_w = width // LRU_HEADS
    hd_ch = hd_w // LANES
    lanes = slice(hd * hd_w, (hd + 1) * hd_w)

    lam = lam_ref[:, lanes]
    neg_c_sp = -LRU_C * (jnp.maximum(-lam, 0.0) + jnp.log1p(jnp.exp(-jnp.abs(lam))))
    xh = xc[:, lanes]
    xb = xh.astype(BF16)
    r = _sigmoid(_dot(xb, wa_ref[hd]) + ba_ref[:, lanes])
    i = _sigmoid(_dot(xb, wx_ref[hd]) + bx_ref[:, lanes])
    log_a = neg_c_sp * r
    a = jnp.exp(log_a)
    t = jnp.tanh(log_a)
    w = -2.0 * t
    mult = jnp.where(w > 0.0, (w * lax.rsqrt(w)) * lax.rsqrt(1.0 - t), 0.0)
    u = (i * xh) * mult

    sub = lax.broadcasted_iota(jnp.int32, (SUBLANES, LANES), 0)
    order = list(range(SUBLANES - 1, -1, -1) if reverse else range(SUBLANES))
    for cc in range(hd_ch):
        c = hd * hd_ch + cc
        for s in range(SUBLANES):
            a_ref[c, s * pitch:s * pitch + seg, :] = a[s * seg:(s + 1) * seg, cc * LANES:(cc + 1) * LANES]
            u_ref[c, s * pitch:s * pitch + seg, :] = u[s * seg:(s + 1) * seg, cc * LANES:(cc + 1) * LANES]
    for c in range(hd * hd_ch, (hd + 1) * hd_ch):
        h = jnp.zeros((SUBLANES, LANES), F32)
        p = jnp.ones((SUBLANES, LANES), F32)
        for it in range(seg):
            k = seg - 1 - it if reverse else it
            av = a_ref[c, pl.ds(k, SUBLANES, stride=pitch), :]
            uv = u_ref[c, pl.ds(k, SUBLANES, stride=pitch), :]
            h = av * h + uv
            p = av * p
            hl_ref[c, k * SUBLANES:(k + 1) * SUBLANES, :] = h
            pc_ref[c, k * SUBLANES:(k + 1) * SUBLANES, :] = p
        cin = carry_ref[c]
        for s, nxt_s in zip(order[:-1], order[1:]):
            cand = p * cin + h
            cin = jnp.where(sub == nxt_s, pltpu.roll(cand, (nxt_s - s) % SUBLANES, 0), cin)
        last_s = order[-1]
        tail = p * cin + h
        carry_ref[c] = jnp.broadcast_to(tail[last_s:last_s + 1, :], (SUBLANES, LANES))
        for s in range(SUBLANES):
            cs = jnp.broadcast_to(cin[s:s + 1, :], (SUBLANES, LANES))
            for k0 in range(seg // SUBLANES):
                src = pl.ds(k0 * SUBLANES * SUBLANES + s, SUBLANES, stride=SUBLANES)
                r0 = s * seg + k0 * SUBLANES
                h_ref[r0:r0 + SUBLANES, c * LANES:(c + 1) * LANES] = hl_ref[c, src, :] + pc_ref[c, src, :] * cs


def _scan_scratch(ts, width):
    nch = width // LANES
    pitch = ts // SUBLANES + SEG_PITCH_PAD
    return [pltpu.VMEM((nch, SUBLANES * pitch, LANES), F32), pltpu.VMEM((nch, SUBLANES * pitch, LANES), F32),
            pltpu.VMEM((nch, ts, LANES), F32), pltpu.VMEM((nch, ts, LANES), F32),
            pltpu.VMEM((nch, SUBLANES, LANES), F32)]


def _mix_fwd_kernel(x_ref, xp_ref, xn_ref, g_ref, wlx_ref, wsc_ref, wsg_ref, cw_ref, cb_ref, wa_ref, ba_ref,
                    wx_ref, bx_ref, lam_ref, scw_ref, wso_ref, ln_g_ref, ln_b_ref, ws_ref, bs_ref, wgo_ref,
                    xc_ref, hf_ref, yb_ref, yc_ref,
                    ext_l_ref, ext_s_ref, a_ref, u_ref, hl_ref, pc_ref, carry_ref, *, n_tiles):
    j = pl.program_id(1)
    tm = x_ref.shape[0]
    sw, gw = wso_ref.shape[0], wgo_ref.shape[0]
    first, last = j == 0, j == n_tiles - 1

    @pl.when(first)
    def _():
        carry_ref[...] = jnp.zeros_like(carry_ref)

    xe = jnp.concatenate([x_ref[...], xp_ref[...], xn_ref[...]], axis=0)
    hn_e = (_rms_unit(xe) * g_ref[...]).astype(BF16)
    hn = hn_e[:tm]

    lx = _dot(hn_e, wlx_ref[...])
    _fill_ext(ext_l_ref, lx[:tm], lx[tm:tm + HALO], lx[tm + HALO:], first, last)
    xc = _conv(ext_l_ref, tm, cw_ref[...], 2) + cb_ref[...]
    xc_ref[...] = xc

    def head(hd):
        _rglru_head(hd, xc, wa_ref, ba_ref, wx_ref, bx_ref, lam_ref, hf_ref, a_ref, u_ref, hl_ref, pc_ref,
                    carry_ref, reverse=False)

    head(0)
    z_sc = _dot(hn_e, wsc_ref[...])
    cx = z_sc[:, sw:2 * sw] * z_sc[:, 2 * sw:]
    _fill_ext(ext_s_ref, cx[:tm], cx[tm:tm + HALO], cx[tm + HALO:], first, last)
    yb_in = (z_sc[:tm, :sw] * _conv(ext_s_ref, tm, scw_ref[...], 1)).astype(BF16)

    head(1)
    zs = _dot(hn, wsg_ref[...])
    u = _gelu(zs[:, :gw])
    v = _gelu(zs[:, gw:])
    vc = v - jnp.mean(v, axis=-1, keepdims=True)
    var = jnp.mean(vc * vc, axis=-1, keepdims=True)
    vn = ((vc * lax.rsqrt(var + EPS)) * ln_g_ref[...] + ln_b_ref[...]).astype(BF16)
    hd_w = gw // SGU_HEADS
    yc_in = []
    for n in range(tm // CHUNK):
        rows = slice(n * CHUNK, (n + 1) * CHUNK)
        mixed = jnp.concatenate(
            [_dot(ws_ref[h], vn[rows, h * hd_w:(h + 1) * hd_w]) for h in range(SGU_HEADS)], axis=-1)
        yc_in.append((u[rows, :] * (mixed + bs_ref[...])).astype(BF16))

    head(2)
    yb_ref[...] = _dot(yb_in, wso_ref[...])
    head(3)
    yc_ref[...] = _dot(jnp.concatenate(yc_in, axis=0), wgo_ref[...])


def _mix_fwd(x, bsz, g, wlx, wsc, wsg, cw, cb, wa, ba, wx, bx, lam, scw, wso, ln_g, ln_b, ws, bs, wgo):
    n, d = x.shape
    lw, sw = wlx.shape[1], wso.shape[0]
    s_len = n // bsz
    tm = min(TM_MIX, s_len)
    nt = s_len // tm
    per = tm // HALO

    def row(w):
        return pl.BlockSpec((tm, w), lambda b, j: (b * nt + j, 0))

    prev = pl.BlockSpec((HALO, d), lambda b, j: (jnp.maximum((b * nt + j) * per - 1, 0), 0))
    nxt = pl.BlockSpec((HALO, d), lambda b, j: (jnp.minimum((b * nt + j + 1) * per, n // HALO - 1), 0))
    consts = (g, wlx, wsc, wsg, cw, cb, wa, ba, wx, bx, lam, scw, wso, ln_g, ln_b, ws, bs, wgo)
    return pl.pallas_call(
        functools.partial(_mix_fwd_kernel, n_tiles=nt),
        grid=(bsz, nt),
        in_specs=[row(d), prev, nxt] + [_const_spec(c.shape) for c in consts],
        out_specs=[row(lw), row(lw), row(d), row(d)],
        out_shape=[jax.ShapeDtypeStruct((n, lw), F32), jax.ShapeDtypeStruct((n, lw), F32),
                   jax.ShapeDtypeStruct((n, d), F32), jax.ShapeDtypeStruct((n, d), F32)],
        scratch_shapes=[pltpu.VMEM((lw // LANES, tm + 2 * HALO, LANES), F32),
                        pltpu.VMEM((sw // LANES, tm + 2 * HALO, LANES), F32)] + _scan_scratch(tm, lw),
        compiler_params=_params("arbitrary", "arbitrary"),
        name="mix_fwd",
    )(x, x, x, *consts)


def _mix_bwd_kernel(x_ref, xc_ref, hf_ref, yb_ref, yc_ref, g_ref, wgt_ref, wmg_ref, wa_ref, ba_ref, wx_ref,
                    bx_ref, lam_ref, wlo_ref, wo_ref, post_g_ref, f_pre_ref, f_wg_ref, f_wu_ref, f_wd_ref,
                    f_post_ref, o_ref, hb_ref, a_ref, u_ref, hl_ref, pc_ref, carry_ref):
    @pl.when(pl.program_id(1) == 0)
    def _():
        carry_ref[...] = jnp.zeros_like(carry_ref)

    d = x_ref.shape[1]
    x = x_ref[...]
    xc = xc_ref[...]
    hn = (_rms_unit(x) * g_ref[...]).astype(BF16)

    def head(hd):
        _rglru_head(hd, xc, wa_ref, ba_ref, wx_ref, bx_ref, lam_ref, hb_ref, a_ref, u_ref, hl_ref, pc_ref,
                    carry_ref, reverse=True)

    head(0)
    gate = _gelu(_dot(hn, wgt_ref[...]))
    head(1)
    g0 = _sigmoid(_dot(hn, wmg_ref[:, 0:d]))
    head(2)
    mbc = _sigmoid(_dot(hn, wmg_ref[:, d:2 * d])) * yb_ref[...]
    head(3)
    mbc = mbc + _sigmoid(_dot(hn, wmg_ref[:, 2 * d:])) * yc_ref[...]

    ya_in = ((hf_ref[...] + hb_ref[...]) * gate).astype(BF16)
    m = g0 * _dot(ya_in, wlo_ref[...]) + mbc
    x2 = x + _rms_unit(_dot(m.astype(BF16), wo_ref[...])) * post_g_ref[...]
    o_ref[...] = _ffn_tile(x2, f_pre_ref, f_wg_ref, f_wu_ref, f_wd_ref, f_post_ref)


def _mix_bwd(x, bsz, xc, hf, yb, yc, g, wgt, wmg, wa, ba, wx, bx, lam, wlo, wo, post_g, f_pre, f_wg, f_wu,
             f_wd, f_post):
    n, d = x.shape
    lw = xc.shape[1]
    s_len = n // bsz
    tm = min(TM_MIX, s_len)
    nt = s_len // tm

    def row(w):
        return pl.BlockSpec((tm, w), lambda b, j: (b * nt + nt - 1 - j, 0))

    consts = (g, wgt, wmg, wa, ba, wx, bx, lam, wlo, wo, post_g, f_pre, f_wg, f_wu, f_wd, f_post)
    return pl.pallas_call(
        _mix_bwd_kernel,
        grid=(bsz, nt),
        in_specs=[row(d), row(lw), row(lw), row(d), row(d)] + [_const_spec(c.shape) for c in consts],
        out_specs=row(d),
        out_shape=jax.ShapeDtypeStruct((n, d), F32),
        scratch_shapes=[pltpu.VMEM((tm, lw), F32)] + _scan_scratch(tm, lw),
        compiler_params=_params("arbitrary", "arbitrary"),
        name="mix_bwd",
    )(x, xc, hf, yb, yc, *consts)


def kernel(x, ffn1_pre_g, ffn1_w_gate, ffn1_w_up, ffn1_w_down, ffn1_post_g, mix_pre_g, w_in, lru_conv_w, lru_conv_b, lru_wa, lru_ba, lru_wx, lru_bx, lru_lambda, lru_w_out, sc_conv_w, sc_w_out, sgu_ln_g, sgu_ln_b, sgu_w_s, sgu_b, sgu_w_out, w_o, mix_post_g, ffn2_pre_g, ffn2_w_gate, ffn2_w_up, ffn2_w_down, ffn2_post_g):
    bsz, s_len, d = x.shape
    depth = w_in.shape[0]
    lw = lru_w_out.shape[1]
    sw = sc_w_out.shape[1]
    gw = sgu_w_out.shape[1]
    assert lw % (LRU_HEADS * LANES) == 0 and sw % LANES == 0 and gw == SGU_HEADS * LANES
    assert all(s_len % min(t, s_len) == 0 for t in (TM_FFN, TM_MIX)) and min(TM_MIX, s_len) % CHUNK == 0
    assert min(TM_MIX, s_len) % (SUBLANES * SUBLANES) == 0
    o_lx, o_sc, o_sg, o_mg = lw, 2 * lw, 2 * lw + 3 * sw, 2 * lw + 3 * sw + 2 * gw

    def vec(p):
        return p.reshape(1, -1)

    bf = lambda p: p.astype(BF16)
    xf = x.reshape(bsz * s_len, d)
    for l in range(depth):
        xf = _ffn(xf, vec(ffn1_pre_g[l]), bf(ffn1_w_gate[l]), bf(ffn1_w_up[l]), bf(ffn1_w_down[l]),
                  vec(ffn1_post_g[l]))
        bs = jnp.repeat(sgu_b[l].T, gw // SGU_HEADS, axis=1)
        wl = w_in[l]
        xc, hf, yb, yc = _mix_fwd(
            xf, bsz, vec(mix_pre_g[l]), bf(wl[:, o_lx:o_sc]), bf(wl[:, o_sc:o_sg]), bf(wl[:, o_sg:o_mg]),
            lru_conv_w[l], vec(lru_conv_b[l]), bf(lru_wa[l, 0]), vec(lru_ba[l, 0]), bf(lru_wx[l, 0]),
            vec(lru_bx[l, 0]), vec(lru_lambda[l, 0]), sc_conv_w[l], bf(sc_w_out[l]), vec(sgu_ln_g[l]),
            vec(sgu_ln_b[l]), bf(sgu_w_s[l]), bs, bf(sgu_w_out[l]))
        xf = _mix_bwd(
            xf, bsz, xc, hf, yb, yc, vec(mix_pre_g[l]), bf(wl[:, :o_lx]), bf(wl[:, o_mg:]), bf(lru_wa[l, 1]),
            vec(lru_ba[l, 1]), bf(lru_wx[l, 1]), vec(lru_bx[l, 1]), vec(lru_lambda[l, 1]), bf(lru_w_out[l]),
            bf(w_o[l]), vec(mix_post_g[l]), vec(ffn2_pre_g[l]), bf(ffn2_w_gate[l]), bf(ffn2_w_up[l]),
            bf(ffn2_w_down[l]), vec(ffn2_post_g[l]))
    return xf.reshape(bsz, s_len, d)
```

```python
import functools

import jax
import jax.numpy as jnp
from jax import lax
from jax.experimental import pallas as pl
from jax.experimental.pallas import tpu as pltpu

F32 = jnp.float32
BF16 = jnp.bfloat16

EPS = 1e-6
LRU_C = 8.0
LRU_HEADS = 4
SGU_HEADS = 4
CHUNK = 128

LANES = 128
SUBLANES = 8
HALO = SUBLANES
VMEM_LIMIT = 56 * 1024 * 1024

TM_FFN = 512
TM_FWD = 512
TM_BWD = 256
SEG_PITCH_PAD = 8
MXU_COLS = 256


def _rms_unit(x):
    return x * lax.rsqrt(jnp.mean(x * x, axis=-1, keepdims=True) + EPS)


def _sigmoid(x):
    return 0.5 * jnp.tanh(0.5 * x) + 0.5


def _silu(x):
    h = 0.5 * x
    return h + h * jnp.tanh(h)


def _gelu(x):
    h = 0.5 * x
    return h + h * jnp.tanh(x * (0.7978845608028654 + 0.035677408136300125 * (x * x)))


def _dot(a, b):
    return jnp.dot(a, b, preferred_element_type=F32)


def _const_spec(shape):
    nd = len(shape)
    return pl.BlockSpec(shape, lambda *_: (0,) * nd, pipeline_mode=pl.Buffered(1))


def _params(*sem):
    return pltpu.CompilerParams(dimension_semantics=sem, vmem_limit_bytes=VMEM_LIMIT)


def _ffn_tile(x, pre_g_ref, wg_ref, wu_ref, wd_ref, post_g_ref, between=()):
    d_ff = wg_ref.shape[1]
    n_chunks = max(len(between), 1)
    step = -(-d_ff // (n_chunks * MXU_COLS)) * MXU_COLS
    hn = (_rms_unit(x) * pre_g_ref[...]).astype(BF16)
    f = None
    for ci, lo in enumerate(range(0, d_ff, step)):
        hi = min(lo + step, d_ff)
        if ci < len(between):
            between[ci]()
        act = (_silu(_dot(hn, wg_ref[:, lo:hi])) * _dot(hn, wu_ref[:, lo:hi])).astype(BF16)
        part = _dot(act, wd_ref[lo:hi, :])
        f = part if f is None else f + part
    for task in between[ci + 1:]:
        task()
    return x + 0.5 * (_rms_unit(f) * post_g_ref[...])


def _ffn_kernel(x_ref, pre_g_ref, wg_ref, wu_ref, wd_ref, post_g_ref, o_ref):
    o_ref[...] = _ffn_tile(x_ref[...], pre_g_ref, wg_ref, wu_ref, wd_ref, post_g_ref)


def _ffn(x, pre_g, wg, wu, wd, post_g):
    n, d = x.shape
    f = wg.shape[1]
    tm = min(TM_FFN, n)
    row = pl.BlockSpec((tm, d), lambda i: (i, 0))
    return pl.pallas_call(
        _ffn_kernel,
        grid=(n // tm,),
        in_specs=[row, _const_spec((1, d)), _const_spec((d, f)), _const_spec((d, f)),
                  _const_spec((f, d)), _const_spec((1, d))],
        out_specs=row,
        out_shape=jax.ShapeDtypeStruct((n, d), F32),
        compiler_params=_params("parallel"),
        name="ffn",
    )(x, pre_g, wg, wu, wd, post_g)


def _fill_ext(ext_ref, main, prev, nxt, first, last):
    rows = main.shape[0]
    for c in range(ext_ref.shape[0]):
        lanes = slice(c * LANES, (c + 1) * LANES)
        ext_ref[c, 0:HALO, :] = jnp.where(first, 0.0, prev[:, lanes])
        ext_ref[c, HALO:HALO + rows, :] = main[:, lanes]
        ext_ref[c, HALO + rows:, :] = jnp.where(last, 0.0, nxt[:, lanes])


def _conv(ext_ref, rows, w, left):
    outs = []
    for c in range(ext_ref.shape[0]):
        acc = None
        for k in range(w.shape[0]):
            term = ext_ref[c, pl.ds(HALO - left + k, rows), :] * w[k:k + 1, c * LANES:(c + 1) * LANES]
            acc = term if acc is None else acc + term
        outs.append(acc)
    return jnp.concatenate(outs, axis=-1)


def _rglru_head(hd, xc_ref, wa_ref, ba_ref, wx_ref, bx_ref, lam_ref, h_ref, a_ref, u_ref, hl_ref, pc_ref,
                carry_ref, *, reverse):
    ts, width = xc_ref.shape
    seg = ts // SUBLANES
    pitch = seg + SEG_PITCH_PAD
    hd_w = width // LRU_HEADS
    hd_ch = hd_w // LANES
    lanes = slice(hd * hd_w, (hd + 1) * hd_w)

    lam = lam_ref[:, lanes]
    neg_c_sp = -LRU_C * (jnp.maximum(-lam, 0.0) + jnp.log1p(jnp.exp(-jnp.abs(lam))))
    xh = xc_ref[:, lanes]
    xb = xh.astype(BF16)
    r = _sigmoid(_dot(xb, wa_ref[hd]) + ba_ref[:, lanes])
    i = _sigmoid(_dot(xb, wx_ref[hd]) + bx_ref[:, lanes])
    log_a = neg_c_sp * r
    a = jnp.exp(log_a)
    t = jnp.tanh(log_a)
    w = -2.0 * t
    mult = jnp.where(w > 0.0, (w * lax.rsqrt(w)) * lax.rsqrt(1.0 - t), 0.0)
    u = (i * xh) * mult

    sub = lax.broadcasted_iota(jnp.int32, (SUBLANES, LANES), 0)
    order = list(range(SUBLANES - 1, -1, -1) if reverse else range(SUBLANES))
    for cc in range(hd_ch):
        for s in range(SUBLANES):
            a_ref[cc, s * pitch:s * pitch + seg, :] = a[s * seg:(s + 1) * seg, cc * LANES:(cc + 1) * LANES]
            u_ref[cc, s * pitch:s * pitch + seg, :] = u[s * seg:(s + 1) * seg, cc * LANES:(cc + 1) * LANES]
    for cc in range(hd_ch):
        c = hd * hd_ch + cc
        h = jnp.zeros((SUBLANES, LANES), F32)
        p = jnp.ones((SUBLANES, LANES), F32)
        for it in range(seg):
            k = seg - 1 - it if reverse else it
            av = a_ref[cc, pl.ds(k, SUBLANES, stride=pitch), :]
            uv = u_ref[cc, pl.ds(k, SUBLANES, stride=pitch), :]
            h = av * h + uv
            p = av * p
            hl_ref[cc, k * SUBLANES:(k + 1) * SUBLANES, :] = h
            pc_ref[cc, k * SUBLANES:(k + 1) * SUBLANES, :] = p
        cin = carry_ref[c]
        for s, nxt_s in zip(order[:-1], order[1:]):
            cand = p * cin + h
            cin = jnp.where(sub == nxt_s, pltpu.roll(cand, (nxt_s - s) % SUBLANES, 0), cin)
        last_s = order[-1]
        tail = p * cin + h
        carry_ref[c] = jnp.broadcast_to(tail[last_s:last_s + 1, :], (SUBLANES, LANES))
        for s in range(SUBLANES):
            cs = jnp.broadcast_to(cin[s:s + 1, :], (SUBLANES, LANES))
            for k0 in range(seg // SUBLANES):
                src = pl.ds(k0 * SUBLANES * SUBLANES + s, SUBLANES, stride=SUBLANES)
                r0 = s * seg + k0 * SUBLANES
                h_ref[r0:r0 + SUBLANES, c * LANES:(c + 1) * LANES] = hl_ref[cc, src, :] + pc_ref[cc, src, :] * cs


def _scan_scratch(ts, width):
    nch = width // LANES
    hd_ch = nch // LRU_HEADS
    pitch = ts // SUBLANES + SEG_PITCH_PAD
    return [pltpu.VMEM((ts, width), F32),
            pltpu.VMEM((hd_ch, SUBLANES * pitch, LANES), F32), pltpu.VMEM((hd_ch, SUBLANES * pitch, LANES), F32),
            pltpu.VMEM((hd_ch, ts, LANES), F32), pltpu.VMEM((hd_ch, ts, LANES), F32),
            pltpu.VMEM((nch, SUBLANES, LANES), F32)]


def _mix_fwd_kernel(x_ref, xp_ref, xn_ref, g_ref, wlx_ref, wsc_ref, wsg_ref, wgt_ref, cw_ref, cb_ref, wa_ref,
                    ba_ref, wx_ref, bx_ref, lam_ref, scw_ref, wso_ref, ln_g_ref, ln_b_ref, ws_ref, bs_ref,
                    wgo_ref, xc_ref, pk_ref,
                    hn_ref, ext_l_ref, ext_s_ref, hf_ref, a_ref, u_ref, hl_ref, pc_ref, carry_ref, *, n_tiles):
    j = pl.program_id(1)
    tm, d = x_ref.shape
    lw, sw, gw = wlx_ref.shape[1], wso_ref.shape[0], wgo_ref.shape[0]
    hd_w = lw // LRU_HEADS
    first, last = j == 0, j == n_tiles - 1

    @pl.when(first)
    def _():
        carry_ref[...] = jnp.zeros_like(carry_ref)

    xe = jnp.concatenate([x_ref[...], xp_ref[...], xn_ref[...]], axis=0)
    hn_ref[...] = (_rms_unit(xe) * g_ref[...]).astype(BF16)

    lx = _dot(hn_ref[...], wlx_ref[...])
    _fill_ext(ext_l_ref, lx[:tm], lx[tm:tm + HALO], lx[tm + HALO:], first, last)
    xc_ref[...] = _conv(ext_l_ref, tm, cw_ref[...], 2) + cb_ref[...]

    def head(hd):
        _rglru_head(hd, xc_ref, wa_ref, ba_ref, wx_ref, bx_ref, lam_ref, hf_ref, a_ref, u_ref, hl_ref, pc_ref,
                    carry_ref, reverse=False)
        lanes = slice(hd * hd_w, (hd + 1) * hd_w)
        pk_ref[:, lanes] = hf_ref[:, lanes].astype(BF16)

    head(0)
    z_sc = _dot(hn_ref[...], wsc_ref[...])
    cx = z_sc[:, sw:2 * sw] * z_sc[:, 2 * sw:]
    _fill_ext(ext_s_ref, cx[:tm], cx[tm:tm + HALO], cx[tm + HALO:], first, last)
    yb_in = (z_sc[:tm, :sw] * _conv(ext_s_ref, tm, scw_ref[...], 1)).astype(BF16)
    pk_ref[:, lw:lw + d] = _dot(yb_in, wso_ref[...]).astype(BF16)

    head(1)
    zs = _dot(hn_ref[:tm, :], wsg_ref[...])
    v = _gelu(zs[:, gw:])
    vc = v - jnp.mean(v, axis=-1, keepdims=True)
    var = jnp.mean(vc * vc, axis=-1, keepdims=True)
    vn = ((vc * lax.rsqrt(var + EPS)) * ln_g_ref[...] + ln_b_ref[...]).astype(BF16)
    u = _gelu(zs[:, :gw])
    sg_w = gw // SGU_HEADS
    yc_in = []
    for n in range(tm // CHUNK):
        rows = slice(n * CHUNK, (n + 1) * CHUNK)
        mixed = jnp.concatenate(
            [_dot(ws_ref[h], vn[rows, h * sg_w:(h + 1) * sg_w]) for h in range(SGU_HEADS)], axis=-1)
        yc_in.append((u[rows, :] * (mixed + bs_ref[...])).astype(BF16))
    pk_ref[:, lw + d:lw + 2 * d] = _dot(jnp.concatenate(yc_in, axis=0), wgo_ref[...]).astype(BF16)

    half = lw // 2
    head(2)
    pk_ref[:, lw + 2 * d:lw + 2 * d + half] = _gelu(_dot(hn_ref[:tm, :], wgt_ref[:, :half])).astype(BF16)
    head(3)
    pk_ref[:, lw + 2 * d + half:] = _gelu(_dot(hn_ref[:tm, :], wgt_ref[:, half:])).astype(BF16)


def _mix_fwd(x, bsz, g, wlx, wsc, wsg, wgt, cw, cb, wa, ba, wx, bx, lam, scw, wso, ln_g, ln_b, ws, bs, wgo):
    n, d = x.shape
    lw, sw = wlx.shape[1], wso.shape[0]
    s_len = n // bsz
    tm = min(TM_FWD, s_len)
    nt = s_len // tm
    per = tm // HALO

    def row(w):
        return pl.BlockSpec((tm, w), lambda b, j: (b * nt + j, 0))

    prev = pl.BlockSpec((HALO, d), lambda b, j: (jnp.maximum((b * nt + j) * per - 1, 0), 0))
    nxt = pl.BlockSpec((HALO, d), lambda b, j: (jnp.minimum((b * nt + j + 1) * per, n // HALO - 1), 0))
    consts = (g, wlx, wsc, wsg, wgt, cw, cb, wa, ba, wx, bx, lam, scw, wso, ln_g, ln_b, ws, bs, wgo)
    pk_w = 2 * lw + 2 * d
    return pl.pallas_call(
        functools.partial(_mix_fwd_kernel, n_tiles=nt),
        grid=(bsz, nt),
        in_specs=[row(d), prev, nxt] + [_const_spec(c.shape) for c in consts],
        out_specs=[row(lw), row(pk_w)],
        out_shape=[jax.ShapeDtypeStruct((n, lw), F32), jax.ShapeDtypeStruct((n, pk_w), BF16)],
        scratch_shapes=[pltpu.VMEM((tm + 2 * HALO, d), BF16),
                        pltpu.VMEM((lw // LANES, tm + 2 * HALO, LANES), F32),
                        pltpu.VMEM((sw // LANES, tm + 2 * HALO, LANES), F32)] + _scan_scratch(tm, lw),
        compiler_params=_params("arbitrary", "arbitrary"),
        name="mix_fwd",
    )(x, x, x, *consts)


def _mix_bwd_kernel(x_ref, xc_ref, pk_ref, g_ref, wmg_ref, wa_ref, ba_ref, wx_ref, bx_ref, lam_ref, wlo_ref,
                    wo_ref, post_g_ref, f_pre_ref, f_wg_ref, f_wu_ref, f_wd_ref, f_post_ref,
                    o_ref, x2_ref, hn_ref, g0_ref, mbc_ref, hb_ref, a_ref, u_ref, hl_ref, pc_ref, carry_ref):
    @pl.when(pl.program_id(1) == 0)
    def _():
        carry_ref[...] = jnp.zeros_like(carry_ref)
        x2_ref[...] = jnp.zeros_like(x2_ref)

    d = x_ref.shape[1]
    lw = xc_ref.shape[1]
    hn_ref[...] = (_rms_unit(x_ref[...]) * g_ref[...]).astype(BF16)

    def task(hd):
        def run():
            _rglru_head(hd, xc_ref, wa_ref, ba_ref, wx_ref, bx_ref, lam_ref, hb_ref, a_ref, u_ref, hl_ref,
                        pc_ref, carry_ref, reverse=True)
            if hd == 0:
                g0_ref[...] = _sigmoid(_dot(hn_ref[...], wmg_ref[:, 0:d]))
            elif hd == 1:
                mbc_ref[...] = (_sigmoid(_dot(hn_ref[...], wmg_ref[:, d:2 * d]))
                                * pk_ref[:, lw:lw + d].astype(F32))
            elif hd == 2:
                mbc_ref[...] += (_sigmoid(_dot(hn_ref[...], wmg_ref[:, 2 * d:]))
                                 * pk_ref[:, lw + d:lw + 2 * d].astype(F32))
        return run

    o_ref[...] = _ffn_tile(x2_ref[...], f_pre_ref, f_wg_ref, f_wu_ref, f_wd_ref, f_post_ref,
                           between=[task(hd) for hd in range(LRU_HEADS)])

    gate = pk_ref[:, lw + 2 * d:].astype(F32)
    ya_in = ((pk_ref[:, :lw].astype(F32) + hb_ref[...]) * gate).astype(BF16)
    m = g0_ref[...] * _dot(ya_in, wlo_ref[...]) + mbc_ref[...]
    x2_ref[...] = x_ref[...] + _rms_unit(_dot(m.astype(BF16), wo_ref[...])) * post_g_ref[...]


def _mix_bwd(x, bsz, xc, pk, g, wmg, wa, ba, wx, bx, lam, wlo, wo, post_g, f_pre, f_wg, f_wu, f_wd, f_post):
    n, d = x.shape
    lw = xc.shape[1]
    s_len = n // bsz
    tm = min(TM_BWD, s_len)
    nt = s_len // tm

    def row_in(w):
        return pl.BlockSpec((tm, w), lambda b, j: (b * nt + nt - 1 - jnp.minimum(j, nt - 1), 0))

    row_out = pl.BlockSpec((tm, d), lambda b, j: (b * nt + nt - jnp.maximum(j, 1), 0))
    consts = (g, wmg, wa, ba, wx, bx, lam, wlo, wo, post_g, f_pre, f_wg, f_wu, f_wd, f_post)
    return pl.pallas_call(
        _mix_bwd_kernel,
        grid=(bsz, nt + 1),
        in_specs=[row_in(d), row_in(lw), row_in(pk.shape[1])] + [_const_spec(c.shape) for c in consts],
        out_specs=row_out,
        out_shape=jax.ShapeDtypeStruct((n, d), F32),
        scratch_shapes=[pltpu.VMEM((tm, d), F32), pltpu.VMEM((tm, d), BF16), pltpu.VMEM((tm, d), F32),
                        pltpu.VMEM((tm, d), F32)] + _scan_scratch(tm, lw),
        compiler_params=_params("arbitrary", "arbitrary"),
        name="mix_bwd",
    )(x, xc, pk, *consts)


def kernel(x, ffn1_pre_g, ffn1_w_gate, ffn1_w_up, ffn1_w_down, ffn1_post_g, mix_pre_g, w_in, lru_conv_w, lru_conv_b, lru_wa, lru_ba, lru_wx, lru_bx, lru_lambda, lru_w_out, sc_conv_w, sc_w_out, sgu_ln_g, sgu_ln_b, sgu_w_s, sgu_b, sgu_w_out, w_o, mix_post_g, ffn2_pre_g, ffn2_w_gate, ffn2_w_up, ffn2_w_down, ffn2_post_g):
    bsz, s_len, d = x.shape
    depth = w_in.shape[0]
    lw = lru_w_out.shape[1]
    sw = sc_w_out.shape[1]
    gw = sgu_w_out.shape[1]
    assert lw % (LRU_HEADS * LANES) == 0 and sw % LANES == 0 and gw == SGU_HEADS * LANES
    for t in (TM_FFN, TM_FWD, TM_BWD):
        assert s_len % min(t, s_len) == 0
    assert min(TM_FWD, s_len) % CHUNK == 0 and min(TM_BWD, s_len) % (SUBLANES * SUBLANES) == 0
    o_lx, o_sc, o_sg, o_mg = lw, 2 * lw, 2 * lw + 3 * sw, 2 * lw + 3 * sw + 2 * gw

    def vec(p):
        return p.reshape(1, -1)

    bf = lambda p: p.astype(BF16)
    xf = x.reshape(bsz * s_len, d)
    for l in range(depth):
        xf = _ffn(xf, vec(ffn1_pre_g[l]), bf(ffn1_w_gate[l]), bf(ffn1_w_up[l]), bf(ffn1_w_down[l]),
                  vec(ffn1_post_g[l]))
        bs = jnp.repeat(sgu_b[l].T, gw // SGU_HEADS, axis=1)
        wl = w_in[l]
        xc, pk = _mix_fwd(
            xf, bsz, vec(mix_pre_g[l]), bf(wl[:, o_lx:o_sc]), bf(wl[:, o_sc:o_sg]), bf(wl[:, o_sg:o_mg]),
            bf(wl[:, :o_lx]), lru_conv_w[l], vec(lru_conv_b[l]), bf(lru_wa[l, 0]), vec(lru_ba[l, 0]),
            bf(lru_wx[l, 0]), vec(lru_bx[l, 0]), vec(lru_lambda[l, 0]), sc_conv_w[l], bf(sc_w_out[l]),
            vec(sgu_ln_g[l]), vec(sgu_ln_b[l]), bf(sgu_w_s[l]), bs, bf(sgu_w_out[l]))
        xf = _mix_bwd(
            xf, bsz, xc, pk, vec(mix_pre_g[l]), bf(wl[:, o_mg:]), bf(lru_wa[l, 1]), vec(lru_ba[l, 1]),
            bf(lru_wx[l, 1]), vec(lru_bx[l, 1]), vec(lru_lambda[l, 1]), bf(lru_w_out[l]), bf(w_o[l]),
            vec(mix_post_g[l]), vec(ffn2_pre_g[l]), bf(ffn2_w_gate[l]), bf(ffn2_w_up[l]), bf(ffn2_w_down[l]),
            vec(ffn2_post_g[l]))
    return xf.reshape(bsz, s_len, d)
```
